```python
import math
import jax, jax.numpy as jnp
from jax import lax
import numpy as np

D_MODEL = 1024
BATCH = 16
SEQ = 2048
DEPTH = 4

CHUNK = 64
LEFT_CHUNKS = 8
BAND = (LEFT_CHUNKS + 1) * CHUNK
ATT_HEADS = 8
ATT_HEAD_DIM = 64
ATT_WIDTH = ATT_HEADS * ATT_HEAD_DIM
REL_CLIP = 256
REL_TABLE = REL_CLIP + CHUNK
CONV_WIDTH = D_MODEL // 2
CONV_K = 31
FFN_DIM = 2816
FFN_CONV_K = 3
N_BRANCH = 2
IN_COLS = 3 * ATT_WIDTH + 2 * CONV_WIDTH + N_BRANCH * D_MODEL
SPLITS = [ATT_WIDTH, 2 * ATT_WIDTH, 3 * ATT_WIDTH, 3 * ATT_WIDTH + 2 * CONV_WIDTH]
ALPHA = (2 * DEPTH) ** 0.25
BETA = (8 * DEPTH) ** -0.25
LN_EPS = 1e-5
NEG_INF = -1e30

kernel_name = "hybrid_chunk_attn_conformer_conv_deepnorm"


def layer_norm(x, g, b):
    xf = x.astype(jnp.float32)
    mu = jnp.mean(xf, axis=-1, keepdims=True)
    xc = xf - mu
    var = jnp.mean(xc * xc, axis=-1, keepdims=True)
    y = xc * lax.rsqrt(var + LN_EPS) * g.astype(jnp.float32) + b.astype(jnp.float32)
    return y.astype(x.dtype)


def causal_dwconv(x, w, b):
    k, c = w.shape
    y = lax.conv_general_dilated(
        x, w[:, None, :].astype(x.dtype), window_strides=(1,),
        padding=((k - 1, 0),), dimension_numbers=("NWC", "WIO", "NWC"),
        feature_group_count=c)
    return y + b.astype(x.dtype)


def chunk_attention(q, k, v, rel_bias):
    bsz, seq, nh, dh = q.shape
    nc = seq // CHUNK
    pad = LEFT_CHUNKS * CHUNK
    kp = jnp.pad(k, ((0, 0), (pad, 0), (0, 0), (0, 0)))
    vp = jnp.pad(v, ((0, 0), (pad, 0), (0, 0), (0, 0)))
    q_ch = q.reshape(bsz, nc, CHUNK, nh, dh).transpose(1, 0, 2, 3, 4)
    qi = jnp.arange(CHUNK)[:, None]
    kj = jnp.arange(BAND)[None, :]
    rel = qi + pad - kj
    idx = jnp.clip(rel, -(CHUNK - 1), REL_CLIP) + (CHUNK - 1)
    bias = rel_bias.astype(jnp.float32)[:, idx]
    scale = 1.0 / math.sqrt(dh)

    def one_chunk(args):
        n, q_n = args
        start = n * CHUNK
        k_n = lax.dynamic_slice_in_dim(kp, start, BAND, axis=1)
        v_n = lax.dynamic_slice_in_dim(vp, start, BAND, axis=1)
        s = jnp.einsum("bqhd,bkhd->bhqk", q_n, k_n).astype(jnp.float32) * scale + bias
        valid = (kj + start) >= pad
        s = jnp.where(valid[None, None], s, NEG_INF)
        p = jax.nn.softmax(s, axis=-1).astype(v.dtype)
        return jnp.einsum("bhqk,bkhd->bqhd", p, v_n)

    out = lax.map(one_chunk, (jnp.arange(nc), q_ch))
    return out.transpose(1, 0, 2, 3, 4).reshape(bsz, seq, nh * dh)


def conformer_conv(u, w_dw, b_dw, g_ln, b_ln):
    a, g = jnp.split(u, 2, axis=-1)
    h = a * jax.nn.sigmoid(g)
    h = causal_dwconv(h, w_dw, b_dw)
    h = layer_norm(h, g_ln, b_ln)
    return jax.nn.silu(h)


def conv_ffn(x, w_up, w_dw, b_dw, w_down):
    u = x @ w_up
    a, b = jnp.split(u, 2, axis=-1)
    a = causal_dwconv(a, w_dw, b_dw)
    return (jax.nn.gelu(a) * b) @ w_down


def setup_inputs(seed: int = 0) -> dict:
    key = jax.random.key(seed)
    ks = jax.random.split(key, 20)
    L, D = DEPTH, D_MODEL

    def nrm(k, shape, fan_in, scale=1.0):
        return jax.random.normal(k, shape, jnp.float32) * (scale * fan_in ** -0.5)

    def small(k, shape, s=0.02):
        return s * jax.random.normal(k, shape, jnp.float32)

    col_scale = jnp.ones((IN_COLS,), jnp.float32).at[2 * ATT_WIDTH:3 * ATT_WIDTH].set(BETA)
    return {
        "x": jax.random.normal(ks[0], (BATCH, SEQ, D), jnp.float32),
        "w_in": nrm(ks[1], (L, D, IN_COLS), D) * col_scale,
        "b_in": small(ks[2], (L, IN_COLS)),
        "rel_bias": small(ks[3], (L, ATT_HEADS, REL_TABLE), 0.1),
        "w_att_out": nrm(ks[4], (L, ATT_WIDTH, D), ATT_WIDTH, BETA),
        "conv_w": nrm(ks[5], (L, CONV_K, CONV_WIDTH), CONV_K),
        "conv_b": small(ks[6], (L, CONV_WIDTH)),
        "conv_ln_g": 1.0 + small(ks[7], (L, CONV_WIDTH)),
        "conv_ln_b": small(ks[8], (L, CONV_WIDTH)),
        "w_conv_out": nrm(ks[9], (L, CONV_WIDTH, D), CONV_WIDTH, BETA),
        "w_o": nrm(ks[10], (L, D, D), D, BETA),
        "ln1_g": 1.0 + small(ks[11], (L, D)),
        "ln1_b": small(ks[12], (L, D)),
        "w_up": nrm(ks[13], (L, D, 2 * FFN_DIM), D, BETA),
        "ffn_conv_w": nrm(ks[14], (L, FFN_CONV_K, FFN_DIM), FFN_CONV_K),
        "ffn_conv_b": small(ks[15], (L, FFN_DIM)),
        "w_down": nrm(ks[16], (L, FFN_DIM, D), FFN_DIM, BETA),
        "ln2_g": 1.0 + small(ks[17], (L, D)),
        "ln2_b": small(ks[18], (L, D)),
    }


def reference(x, w_in, b_in, rel_bias, w_att_out, conv_w, conv_b, conv_ln_g, conv_ln_b,
              w_conv_out, w_o, ln1_g, ln1_b, w_up, ffn_conv_w, ffn_conv_b, w_down,
              ln2_g, ln2_b):
    bsz, seq, _ = x.shape
    for l in range(DEPTH):
        h = x @ w_in[l] + b_in[l]
        q, k, v, conv_in, gate_logits = jnp.split(h, SPLITS, axis=-1)
        q = q.reshape(bsz, seq, ATT_HEADS, ATT_HEAD_DIM)
        k = k.reshape(bsz, seq, ATT_HEADS, ATT_HEAD_DIM)
        v = v.reshape(bsz, seq, ATT_HEADS, ATT_HEAD_DIM)
        y_att = chunk_attention(q, k, v, rel_bias[l]) @ w_att_out[l]
        y_conv = conformer_conv(conv_in, conv_w[l], conv_b[l], conv_ln_g[l],
                                conv_ln_b[l]) @ w_conv_out[l]
        g_att, g_conv = jnp.split(jax.nn.sigmoid(gate_logits), N_BRANCH, axis=-1)
        mix = (g_att * y_att + g_conv * y_conv) @ w_o[l]
        x = layer_norm(ALPHA * x + mix, ln1_g[l], ln1_b[l])
        ffn = conv_ffn(x, w_up[l], ffn_conv_w[l], ffn_conv_b[l], w_down[l])
        x = layer_norm(ALPHA * x + ffn, ln2_g[l], ln2_b[l])
    return x
```

```python
import functools
import math

import jax
import jax.numpy as jnp
from jax import lax
from jax.experimental import pallas as pl
from jax.experimental.pallas import tpu as pltpu

D_MODEL = 1024
DEPTH = 4
CHUNK = 64
LEFT_CHUNKS = 8
BAND = (LEFT_CHUNKS + 1) * CHUNK
ATT_HEADS = 8
ATT_HEAD_DIM = 64
ATT_WIDTH = ATT_HEADS * ATT_HEAD_DIM
REL_CLIP = 256
CONV_WIDTH = D_MODEL // 2
CONV_K = 31
FFN_DIM = 2816
FFN_CONV_K = 3
IN_COLS = 3 * ATT_WIDTH + 2 * CONV_WIDTH + 2 * D_MODEL
ALPHA = (2 * DEPTH) ** 0.25
LN_EPS = 1e-5
NEG_INF = -1e30

LANES = 128
SUBLANES = 8
TS = 512
PAIR = 2 * CHUNK
WIN = BAND + CHUNK
CONV_HALO = 32
CONV_OFF = CONV_HALO - (CONV_K - 1)
CONV_RB = 32
FFN_HALO = SUBLANES
FFN_CH = 256
VMEM_LIMIT = 56 * 1024 * 1024

_F32 = jnp.float32
_BF16 = jnp.bfloat16


def _layer_norm(y, g, b):
    mu = jnp.mean(y, axis=-1, keepdims=True)
    yc = y - mu
    var = jnp.mean(yc * yc, axis=-1, keepdims=True)
    return yc * lax.rsqrt(var + LN_EPS) * g + b


def _mixer_kernel(x_ref, w_in_ref, b_in_ref, bias_ref, w_att_ref, w_conv_ref, w_o_ref,
                  cw_ref, cb_ref, clg_ref, clb_ref, g1_ref, b1_ref,
                  o_ref,
                  q_s, k_s, v_s, c_s, gate_s, att_s, cact_s):
    t = pl.program_id(1)
    first = t == 0

    @pl.when(first)
    def _():
        k_s[0:TS, :] = jnp.zeros((TS, ATT_WIDTH), _BF16)
        v_s[0:TS, :] = jnp.zeros((TS, ATT_WIDTH), _BF16)
        c_s[0:CONV_HALO, :] = jnp.zeros((CONV_HALO, CONV_WIDTH), _F32)

    @pl.when(t > 0)
    def _():
        k_s[0:TS, :] = k_s[TS:2 * TS, :]
        v_s[0:TS, :] = v_s[TS:2 * TS, :]
        c_s[0:CONV_HALO, :] = c_s[TS:TS + CONV_HALO, :]

    xb = x_ref[0].astype(_BF16)

    def proj(c0, width):
        h = jnp.dot(xb, w_in_ref[:, c0:c0 + width], preferred_element_type=_F32)
        return h + b_in_ref[:, c0:c0 + width]

    q_s[...] = (proj(0, ATT_WIDTH) * (1.0 / math.sqrt(ATT_HEAD_DIM))).astype(_BF16)
    k_s[TS:2 * TS, :] = proj(ATT_WIDTH, ATT_WIDTH).astype(_BF16)
    v_s[TS:2 * TS, :] = proj(2 * ATT_WIDTH, ATT_WIDTH).astype(_BF16)
    c0 = 3 * ATT_WIDTH
    c_s[CONV_HALO:CONV_HALO + TS, :] = (
        proj(c0, CONV_WIDTH) * jax.nn.sigmoid(proj(c0 + CONV_WIDTH, CONV_WIDTH)))
    c0 += 2 * CONV_WIDTH
    for j in range(2 * D_MODEL // 512):
        gate_s[:, 512 * j:512 * (j + 1)] = jax.nn.sigmoid(proj(c0 + 512 * j, 512))

    low_half = lax.broadcasted_iota(jnp.int32, (PAIR, LANES), 1) < ATT_HEAD_DIM
    for i in range(TS // PAIR):
        r0 = PAIR * i
        stale_cols = TS - r0
        for hp in range(ATT_HEADS // 2):
            cs = slice(LANES * hp, LANES * (hp + 1))
            qp = q_s[r0:r0 + PAIR, cs]
            kw = k_s[r0:r0 + WIN, cs]
            vw = v_s[r0:r0 + WIN, cs]
            outs = []
            for sub in range(2):
                qm = jnp.where(low_half if sub == 0 else jnp.logical_not(low_half), qp,
                               jnp.zeros_like(qp))
                s = lax.dot_general(qm, kw, (((1,), (1,)), ((), ())),
                                    preferred_element_type=_F32)
                s = s + bias_ref[2 * hp + sub]
                s = jnp.concatenate(
                    [jnp.where(first, NEG_INF, s[:, :stale_cols]), s[:, stale_cols:]], axis=1)
                m = jnp.max(s, axis=-1, keepdims=True)
                p = jnp.exp(s - m)
                l = jnp.sum(p, axis=-1, keepdims=True)
                o = jnp.dot(p.astype(_BF16), vw, preferred_element_type=_F32)
                outs.append(o / l)
            att_s[r0:r0 + PAIR, cs] = jnp.where(low_half, outs[0], outs[1]).astype(_BF16)

    cb = cb_ref[...]
    clg = clg_ref[...]
    clb = clb_ref[...]
    for rb in range(TS // CONV_RB):
        base = CONV_RB * rb + CONV_OFF
        acc = jnp.zeros((CONV_RB, CONV_WIDTH), _F32)
        for k in range(CONV_K):
            acc = acc + c_s[base + k:base + k + CONV_RB, :] * cw_ref[k:k + 1, :]
        hn = _layer_norm(acc + cb, clg, clb)
        cact_s[CONV_RB * rb:CONV_RB * (rb + 1), :] = (hn * jax.nn.sigmoid(hn)).astype(_BF16)

    y_att = jnp.dot(att_s[...], w_att_ref[...], preferred_element_type=_F32)
    y_conv = jnp.dot(cact_s[...], w_conv_ref[...], preferred_element_type=_F32)
    merged = gate_s[:, 0:D_MODEL] * y_att + gate_s[:, D_MODEL:2 * D_MODEL] * y_conv
    mix = jnp.dot(merged.astype(_BF16), w_o_ref[...], preferred_element_type=_F32)
    o_ref[0] = _layer_norm(ALPHA * x_ref[0] + mix, g1_ref[...], b1_ref[...])


def _ffn_kernel(x_ref, w_up_ref, fw_ref, fb_ref, w_down_ref, g2_ref, b2_ref,
                o_ref,
                a_s, tail_s, acc_s):
    t = pl.program_id(1)

    @pl.when(t == 0)
    def _():
        tail_s[...] = jnp.zeros(tail_s.shape, _F32)

    xb = x_ref[0].astype(_BF16)
    for j in range(FFN_DIM // FFN_CH):
        cs = slice(FFN_CH * j, FFN_CH * (j + 1))
        a = jnp.dot(xb, w_up_ref[:, cs], preferred_element_type=_F32)
        b = jnp.dot(xb, w_up_ref[:, FFN_DIM + FFN_CH * j:FFN_DIM + FFN_CH * (j + 1)],
                    preferred_element_type=_F32)
        a_s[0:FFN_HALO, :] = tail_s[:, cs]
        a_s[FFN_HALO:FFN_HALO + TS, :] = a
        tail_s[:, cs] = a[TS - FFN_HALO:TS, :]
        conv = fb_ref[:, cs]
        for k in range(FFN_CONV_K):
            off = FFN_HALO - (FFN_CONV_K - 1) + k
            conv = conv + a_s[off:off + TS, :] * fw_ref[k:k + 1, cs]
        hidden = (jax.nn.gelu(conv) * b).astype(_BF16)
        part = jnp.dot(hidden, w_down_ref[cs, :], preferred_element_type=_F32)
        if j == 0:
            acc_s[...] = part
        else:
            acc_s[...] += part
    o_ref[0] = _layer_norm(ALPHA * x_ref[0] + acc_s[...], g2_ref[...], b2_ref[...])


def _resident(shape):
    return pl.BlockSpec(shape, lambda b, t: (0,) * len(shape), pipeline_mode=pl.Buffered(1))


def _token_tile():
    return pl.BlockSpec((1, TS, D_MODEL), lambda b, t: (b, t, 0))


def _mixer_call(bsz, seq):
    return pl.pallas_call(
        _mixer_kernel,
        grid=(bsz, seq // TS),
        in_specs=[
            _token_tile(),
            _resident((D_MODEL, IN_COLS)),
            _resident((1, IN_COLS)),
            _resident((ATT_HEADS, PAIR, WIN)),
            _resident((ATT_WIDTH, D_MODEL)),
            _resident((CONV_WIDTH, D_MODEL)),
            _resident((D_MODEL, D_MODEL)),
            _resident((CONV_K + 1, CONV_WIDTH)),
            _resident((1, CONV_WIDTH)),
            _resident((1, CONV_WIDTH)),
            _resident((1, CONV_WIDTH)),
            _resident((1, D_MODEL)),
            _resident((1, D_MODEL)),
        ],
        out_specs=_token_tile(),
        out_shape=jax.ShapeDtypeStruct((bsz, seq, D_MODEL), _F32),
        scratch_shapes=[
            pltpu.VMEM((TS, ATT_WIDTH), _BF16),
            pltpu.VMEM((2 * TS, ATT_WIDTH), _BF16),
            pltpu.VMEM((2 * TS, ATT_WIDTH), _BF16),
            pltpu.VMEM((CONV_HALO + TS, CONV_WIDTH), _F32),
            pltpu.VMEM((TS, 2 * D_MODEL), _F32),
            pltpu.VMEM((TS, ATT_WIDTH), _BF16),
            pltpu.VMEM((TS, CONV_WIDTH), _BF16),
        ],
        compiler_params=pltpu.CompilerParams(
            dimension_semantics=("arbitrary", "arbitrary"),
            vmem_limit_bytes=VMEM_LIMIT),
        name="mixer",
    )


def _ffn_call(bsz, seq):
    return pl.pallas_call(
        _ffn_kernel,
        grid=(bsz, seq // TS),
        in_specs=[
            _token_tile(),
            _resident((D_MODEL, 2 * FFN_DIM)),
            _resident((FFN_CONV_K, FFN_DIM)),
            _resident((1, FFN_DIM)),
            _resident((FFN_DIM, D_MODEL)),
            _resident((1, D_MODEL)),
            _resident((1, D_MODEL)),
        ],
        out_specs=_token_tile(),
        out_shape=jax.ShapeDtypeStruct((bsz, seq, D_MODEL), _F32),
        scratch_shapes=[
            pltpu.VMEM((FFN_HALO + TS, FFN_CH), _F32),
            pltpu.VMEM((FFN_HALO, FFN_DIM), _F32),
            pltpu.VMEM((TS, D_MODEL), _F32),
        ],
        compiler_params=pltpu.CompilerParams(
            dimension_semantics=("arbitrary", "arbitrary"),
            vmem_limit_bytes=VMEM_LIMIT),
        name="ffn",
    )


def _pair_bias(rel_bias):
    r = jnp.arange(PAIR)[:, None]
    c = jnp.arange(WIN)[None, :]
    parity = r // CHUNK
    kj = c - CHUNK * parity
    rel = (r % CHUNK) + LEFT_CHUNKS * CHUNK - kj
    idx = jnp.clip(rel, -(CHUNK - 1), REL_CLIP) + (CHUNK - 1)
    in_band = (kj >= 0) & (kj < BAND)
    return jnp.where(in_band[None], rel_bias.astype(_F32)[:, idx], NEG_INF)


def kernel(x, w_in, b_in, rel_bias, w_att_out, conv_w, conv_b, conv_ln_g, conv_ln_b,
           w_conv_out, w_o, ln1_g, ln1_b, w_up, ffn_conv_w, ffn_conv_b, w_down,
           ln2_g, ln2_b):
    bsz, seq, d = x.shape
    assert d == D_MODEL and seq % TS == 0 and TS == LEFT_CHUNKS * CHUNK
    mixer = _mixer_call(bsz, seq)
    ffn = _ffn_call(bsz, seq)
    row = lambda v: v.reshape(1, -1)
    for l in range(DEPTH):
        conv_w_pad = jnp.pad(conv_w[l], ((0, 1), (0, 0)))
        x = mixer(x, w_in[l].astype(_BF16), row(b_in[l]), _pair_bias(rel_bias[l]),
                  w_att_out[l].astype(_BF16), w_conv_out[l].astype(_BF16),
                  w_o[l].astype(_BF16), conv_w_pad, row(conv_b[l]), row(conv_ln_g[l]),
                  row(conv_ln_b[l]), row(ln1_g[l]), row(ln1_b[l]))
        x = ffn(x, w_up[l].astype(_BF16), ffn_conv_w[l], row(ffn_conv_b[l]),
                w_down[l].astype(_BF16), row(ln2_g[l]), row(ln2_b[l]))
    return x
```

```python
import math

import jax
import jax.numpy as jnp
from jax import lax
from jax.experimental import pallas as pl
from jax.experimental.pallas import tpu as pltpu

D_MODEL = 1024
DEPTH = 4
CHUNK = 64
LEFT_CHUNKS = 8
BAND = (LEFT_CHUNKS + 1) * CHUNK
ATT_HEADS = 8
ATT_HEAD_DIM = 64
ATT_WIDTH = ATT_HEADS * ATT_HEAD_DIM
REL_CLIP = 256
CONV_WIDTH = D_MODEL // 2
CONV_K = 31
FFN_DIM = 2816
FFN_CONV_K = 3
IN_COLS = 3 * ATT_WIDTH + 2 * CONV_WIDTH + 2 * D_MODEL
ALPHA = (2 * DEPTH) ** 0.25
LN_EPS = 1e-5
NEG_INF = -1e30

LANES = 128
SUBLANES = 8
MXU_COLS = 256
TS = 512
PAIR = 2 * CHUNK
WIN = BAND + CHUNK
CONV_HALO = 32
CONV_OFF = CONV_HALO - (CONV_K - 1)
CONV_RB = 32
CONV_TILES = CONV_WIDTH // LANES
FFN_HALO = SUBLANES
FFN_CH = MXU_COLS
FFN_CH_TILES = FFN_CH // LANES
VMEM_LIMIT = 56 * 1024 * 1024

_F32 = jnp.float32
_BF16 = jnp.bfloat16


def _layer_norm(y, g, b):
    mu = jnp.mean(y, axis=-1, keepdims=True)
    yc = y - mu
    var = jnp.mean(yc * yc, axis=-1, keepdims=True)
    return yc * lax.rsqrt(var + LN_EPS) * g + b


def _interleave(major, minor):
    done = 0
    for n, step in enumerate(major):
        while done < len(minor) and done * len(major) <= n * len(minor):
            minor[done]()
            done += 1
        step()
    for step in minor[done:]:
        step()


def _mixer_kernel(x_ref, w_in_ref, b_in_ref, bias_ref, w_att_ref, w_conv_ref, w_o_ref,
                  cw_ref, cb_ref, clg_ref, clb_ref, g1_ref, b1_ref,
                  o_ref,
                  xb_s, q_s, kt_s, v_s, c_s, gate_s, s_s, p_s, att_s, cact_s, merged_s):
    t = pl.program_id(1)
    first = t == 0

    @pl.when(first)
    def _():
        kt_s[:, 0:TS] = jnp.zeros((ATT_WIDTH, TS), _BF16)
        v_s[0:TS, :] = jnp.zeros((TS, ATT_WIDTH), _BF16)
        c_s[:, 0:CONV_HALO, :] = jnp.zeros((CONV_TILES, CONV_HALO, LANES), _F32)

    @pl.when(t > 0)
    def _():
        kt_s[:, 0:TS] = kt_s[:, TS:2 * TS]
        v_s[0:TS, :] = v_s[TS:2 * TS, :]
        c_s[:, 0:CONV_HALO, :] = c_s[:, TS:TS + CONV_HALO, :]

    xb_s[...] = x_ref[0].astype(_BF16)

    def proj(c0, width):
        h = jnp.dot(xb_s[...], w_in_ref[:, c0:c0 + width], preferred_element_type=_F32)
        return h + b_in_ref[:, c0:c0 + width]

    conv_c0 = 3 * ATT_WIDTH
    glu = proj(conv_c0, CONV_WIDTH) * jax.nn.sigmoid(proj(conv_c0 + CONV_WIDTH, CONV_WIDTH))
    for ct in range(CONV_TILES):
        c_s[ct, CONV_HALO:CONV_HALO + TS, :] = glu[:, LANES * ct:LANES * (ct + 1)]

    q_s[...] = (proj(0, ATT_WIDTH) * (1.0 / math.sqrt(ATT_HEAD_DIM))).astype(_BF16)
    kt_s[:, TS:2 * TS] = proj(ATT_WIDTH, ATT_WIDTH).T.astype(_BF16)
    v_s[TS:2 * TS, :] = proj(2 * ATT_WIDTH, ATT_WIDTH).astype(_BF16)

    gate_c0 = conv_c0 + 2 * CONV_WIDTH

    def proj_gate(j):
        def step():
            cs = slice(512 * j, 512 * (j + 1))
            gate_s[:, cs] = jax.nn.sigmoid(proj(gate_c0 + 512 * j, 512))
        return step

    cb = cb_ref[...]
    clg = clg_ref[...]
    clb = clb_ref[...]

    def conv_block(rb):
        def step():
            base = CONV_RB * rb + CONV_OFF
            accs = []
            for ct in range(CONV_TILES):
                lanes = slice(LANES * ct, LANES * (ct + 1))
                acc = jnp.zeros((CONV_RB, LANES), _F32)
                for k in range(CONV_K):
                    acc = acc + c_s[ct, base + k:base + k + CONV_RB, :] * cw_ref[k:k + 1, lanes]
                accs.append(acc)
            hn = _layer_norm(jnp.concatenate(accs, axis=1) + cb, clg, clb)
            cact_s[CONV_RB * rb:CONV_RB * (rb + 1), :] = (hn * jax.nn.sigmoid(hn)).astype(_BF16)
        return step

    _interleave([conv_block(rb) for rb in range(TS // CONV_RB)],
                [proj_gate(j) for j in range(2 * D_MODEL // 512)])

    low_half = lax.broadcasted_iota(jnp.int32, (PAIR, LANES), 1) < ATT_HEAD_DIM

    def attention_pair(i):
        def step():
            r0 = PAIR * i
            stale_cols = TS - r0
            for h in range(ATT_HEADS):
                hp, sub = divmod(h, 2)
                cs = slice(LANES * hp, LANES * (hp + 1))
                qp = q_s[r0:r0 + PAIR, cs]
                qm = jnp.where(low_half if sub == 0 else jnp.logical_not(low_half), qp,
                               jnp.zeros_like(qp))
                s_s[h] = (jnp.dot(qm, kt_s[cs, r0:r0 + WIN], preferred_element_type=_F32)
                          + bias_ref[h])
            s = s_s[...]
            s = jnp.concatenate(
                [jnp.where(first, NEG_INF, s[:, :, :stale_cols]), s[:, :, stale_cols:]], axis=2)
            m = jnp.max(s, axis=-1, keepdims=True)
            p = jnp.exp(s - m)
            inv_l = 1.0 / jnp.sum(p, axis=-1, keepdims=True)
            p_s[...] = p.astype(_BF16)
            for hp in range(ATT_HEADS // 2):
                cs = slice(LANES * hp, LANES * (hp + 1))
                vw = v_s[r0:r0 + WIN, cs]
                o0 = jnp.dot(p_s[2 * hp], vw, preferred_element_type=_F32) * inv_l[2 * hp]
                o1 = jnp.dot(p_s[2 * hp + 1], vw, preferred_element_type=_F32) * inv_l[2 * hp + 1]
                att_s[r0:r0 + PAIR, cs] = jnp.where(low_half, o0, o1).astype(_BF16)
        return step

    def conv_out(j):
        def step():
            cs = slice(MXU_COLS * j, MXU_COLS * (j + 1))
            y_conv = jnp.dot(cact_s[...], w_conv_ref[:, cs], preferred_element_type=_F32)
            merged_s[:, cs] = gate_s[:, D_MODEL + MXU_COLS * j:D_MODEL + MXU_COLS * (j + 1)] * y_conv
        return step

    _interleave([attention_pair(i) for i in range(TS // PAIR)],
                [conv_out(j) for j in range(D_MODEL // MXU_COLS)])

    y_att = jnp.dot(att_s[...], w_att_ref[...], preferred_element_type=_F32)
    merged = gate_s[:, 0:D_MODEL] * y_att + merged_s[...]
    mix = jnp.dot(merged.astype(_BF16), w_o_ref[...], preferred_element_type=_F32)
    o_ref[0] = _layer_norm(ALPHA * x_ref[0] + mix, g1_ref[...], b1_ref[...])


def _ffn_kernel(x_ref, w_up_ref, fw_ref, fb_ref, w_down_ref, g2_ref, b2_ref,
                o_ref,
                xb_s, a_s, acc_s):
    t = pl.program_id(1)

    @pl.when(t == 0)
    def _():
        a_s[:, 0:FFN_HALO, :] = jnp.zeros((FFN_DIM // LANES, FFN_HALO, LANES), _F32)

    xb_s[...] = x_ref[0].astype(_BF16)
    n_steps = FFN_DIM // FFN_CH

    def up(j):
        a = jnp.dot(xb_s[...], w_up_ref[:, FFN_CH * j:FFN_CH * (j + 1)],
                    preferred_element_type=_F32)
        b = jnp.dot(xb_s[...], w_up_ref[:, FFN_DIM + FFN_CH * j:FFN_DIM + FFN_CH * (j + 1)],
                    preferred_element_type=_F32)
        return a, b

    nxt = up(0)
    for j in range(n_steps):
        a, b = nxt
        if j + 1 < n_steps:
            nxt = up(j + 1)
        cs = slice(FFN_CH * j, FFN_CH * (j + 1))
        tiles = range(FFN_CH_TILES * j, FFN_CH_TILES * (j + 1))
        for n, ct in enumerate(tiles):
            a_s[ct, FFN_HALO:FFN_HALO + TS, :] = a[:, LANES * n:LANES * (n + 1)]
        conv = fb_ref[:, cs]
        for k in range(FFN_CONV_K):
            off = FFN_HALO - (FFN_CONV_K - 1) + k
            window = jnp.concatenate([a_s[ct, off:off + TS, :] for ct in tiles], axis=1)
            conv = conv + window * fw_ref[k:k + 1, cs]
        for n, ct in enumerate(tiles):
            a_s[ct, 0:FFN_HALO, :] = a[TS - FFN_HALO:TS, LANES * n:LANES * (n + 1)]
        hidden = (jax.nn.gelu(conv) * b).astype(_BF16)
        part = jnp.dot(hidden, w_down_ref[cs, :], preferred_element_type=_F32)
        if j == 0:
            acc_s[...] = part
        else:
            acc_s[...] += part
    o_ref[0] = _layer_norm(ALPHA * x_ref[0] + acc_s[...], g2_ref[...], b2_ref[...])


def _resident(shape):
    return pl.BlockSpec(shape, lambda b, t: (0,) * len(shape), pipeline_mode=pl.Buffered(1))


def _token_tile():
    return pl.BlockSpec((1, TS, D_MODEL), lambda b, t: (b, t, 0))


def _mixer_call(bsz, seq):
    return pl.pallas_call(
        _mixer_kernel,
        grid=(bsz, seq // TS),
        in_specs=[
            _token_tile(),
            _resident((D_MODEL, IN_COLS)),
            _resident((1, IN_COLS)),
            _resident((ATT_HEADS, PAIR, WIN)),
            _resident((ATT_WIDTH, D_MODEL)),
            _resident((CONV_WIDTH, D_MODEL)),
            _resident((D_MODEL, D_MODEL)),
            _resident((CONV_K + 1, CONV_WIDTH)),
            _resident((1, CONV_WIDTH)),
            _resident((1, CONV_WIDTH)),
            _resident((1, CONV_WIDTH)),
            _resident((1, D_MODEL)),
            _resident((1, D_MODEL)),
        ],
        out_specs=_token_tile(),
        out_shape=jax.ShapeDtypeStruct((bsz, seq, D_MODEL), _F32),
        scratch_shapes=[
            pltpu.VMEM((TS, D_MODEL), _BF16),
            pltpu.VMEM((TS, ATT_WIDTH), _BF16),
            pltpu.VMEM((ATT_WIDTH, 2 * TS), _BF16),
            pltpu.VMEM((2 * TS, ATT_WIDTH), _BF16),
            pltpu.VMEM((CONV_TILES, CONV_HALO + TS, LANES), _F32),
            pltpu.VMEM((TS, 2 * D_MODEL), _F32),
            pltpu.VMEM((ATT_HEADS, PAIR, WIN), _F32),
            pltpu.VMEM((ATT_HEADS, PAIR, WIN), _BF16),
            pltpu.VMEM((TS, ATT_WIDTH), _BF16),
            pltpu.VMEM((TS, CONV_WIDTH), _BF16),
            pltpu.VMEM((TS, D_MODEL), _F32),
        ],
        compiler_params=pltpu.CompilerParams(
            dimension_semantics=("arbitrary", "arbitrary"),
            vmem_limit_bytes=VMEM_LIMIT),
        name="mixer",
    )


def _ffn_call(bsz, seq):
    return pl.pallas_call(
        _ffn_kernel,
        grid=(bsz, seq // TS),
        in_specs=[
            _token_tile(),
            _resident((D_MODEL, 2 * FFN_DIM)),
            _resident((FFN_CONV_K, FFN_DIM)),
            _resident((1, FFN_DIM)),
            _resident((FFN_DIM, D_MODEL)),
            _resident((1, D_MODEL)),
            _resident((1, D_MODEL)),
        ],
        out_specs=_token_tile(),
        out_shape=jax.ShapeDtypeStruct((bsz, seq, D_MODEL), _F32),
        scratch_shapes=[
            pltpu.VMEM((TS, D_MODEL), _BF16),
            pltpu.VMEM((FFN_DIM // LANES, FFN_HALO + TS, LANES), _F32),
            pltpu.VMEM((TS, D_MODEL), _F32),
        ],
        compiler_params=pltpu.CompilerParams(
            dimension_semantics=("arbitrary", "arbitrary"),
            vmem_limit_bytes=VMEM_LIMIT),
        name="ffn",
    )


def _pair_bias(rel_bias):
    table = rel_bias.astype(_F32)
    diag = PAIR + WIN - 1
    n_floor = PAIR - 1 - (CHUNK - 1)
    n_ceil = diag - n_floor - table.shape[1]
    by_dist = jnp.concatenate([jnp.repeat(table[:, :1], n_floor, axis=1), table,
                               jnp.repeat(table[:, -1:], n_ceil, axis=1)], axis=1)
    slid = jnp.roll(by_dist[:, ::-1], -(PAIR - 1), axis=1)
    toeplitz = jnp.tile(slid, (1, PAIR))[:, :PAIR * (diag - 1)].reshape(-1, PAIR, diag - 1)
    r = jnp.arange(PAIR)[:, None]
    c = jnp.arange(WIN)[None, :]
    band_start = CHUNK * (r // CHUNK)
    in_band = (c >= band_start) & (c < band_start + BAND)
    return jnp.where(in_band[None], toeplitz[:, :, :WIN], NEG_INF)


def kernel(x, w_in, b_in, rel_bias, w_att_out, conv_w, conv_b, conv_ln_g, conv_ln_b,
           w_conv_out, w_o, ln1_g, ln1_b, w_up, ffn_conv_w, ffn_conv_b, w_down,
           ln2_g, ln2_b):
    bsz, seq, d = x.shape
    assert d == D_MODEL and seq % TS == 0 and TS == LEFT_CHUNKS * CHUNK
    mixer = _mixer_call(bsz, seq)
    ffn = _ffn_call(bsz, seq)
    row = lambda v: v.reshape(1, -1)
    for l in range(DEPTH):
        conv_w_pad = jnp.pad(conv_w[l], ((0, 1), (0, 0)))
        x = mixer(x, w_in[l].astype(_BF16), row(b_in[l]), _pair_bias(rel_bias[l]),
                  w_att_out[l].astype(_BF16), w_conv_out[l].astype(_BF16),
                  w_o[l].astype(_BF16), conv_w_pad, row(conv_b[l]), row(conv_ln_g[l]),
                  row(conv_ln_b[l]), row(ln1_g[l]), row(ln1_b[l]))
        x = ffn(x, w_up[l].astype(_BF16), ffn_conv_w[l], row(ffn_conv_b[l]),
                w_down[l].astype(_BF16), row(ln2_g[l]), row(ln2_b[l]))
    return x
```
